```python
import math
import jax
import jax.numpy as jnp
from jax import lax
import numpy as np

D_MODEL = 4096
BATCH = 4
SEQ = 2048
DEPTH = 2
DEC_BATCH = 8
DEC_SEQ = 8
PAST_LEN = 16384
PAGE_SIZE = 128

HEAD_DIM = 128
HALF_DIM = HEAD_DIM // 2
MIX_HEADS = D_MODEL // HEAD_DIM
HEADS_C = MIX_HEADS // 4
HEADS_A = (MIX_HEADS - HEADS_C) // 2
HEADS_B = MIX_HEADS - HEADS_A - HEADS_C
KVH_A = HEADS_A // 3
KVH_B = HEADS_B // 3
KVH_C = HEADS_C // 2
KV_HEADS = KVH_A + KVH_B + KVH_C
IDX_HEADS = 32
IDX_DIM = 128
IDX_ROPE_DIM = 64
DSA_TOPK = 256
MOBA_BLOCK = 256
MOBA_TOPK = 3
MEM_TOKENS = 256
MEM_HEADS = 4
D_FF = 4 * D_MODEL
ROPE_THETA = 10000.0
NORM_EPS = 1e-6
ATTN_QBLOCK = 128
MOBA_QBLOCK = 16
IN_WIDTHS = (HEADS_A * HEAD_DIM, HEADS_B * HEAD_DIM, HEADS_C * HEAD_DIM,
             KV_HEADS * HEAD_DIM, KV_HEADS * HEAD_DIM,
             IDX_HEADS * IDX_DIM, IDX_HEADS, IDX_DIM)
IN_DIM = sum(IN_WIDTHS)
IN_SPLITS = tuple(int(c) for c in np.cumsum(IN_WIDTHS)[:-1])

kernel_name = 'hybrid_dsa_moba_diff_decode_step'


def rms_norm(x, g):
    xf = x.astype(jnp.float32)
    y = xf * lax.rsqrt(jnp.mean(xf * xf, axis=-1, keepdims=True) + NORM_EPS)
    return (y * g.astype(jnp.float32)).astype(x.dtype)


def rope(x, pos):
    half = x.shape[-1] // 2
    inv = ROPE_THETA ** (-jnp.arange(half, dtype=jnp.float32) / half)
    ang = pos.astype(jnp.float32)[:, None] * inv[None, :]
    shape = (pos.shape[0],) + (1,) * (x.ndim - 3) + (half,)
    cos = jnp.cos(ang).reshape(shape)
    sin = jnp.sin(ang).reshape(shape)
    xf = x.astype(jnp.float32)
    x1, x2 = xf[..., :half], xf[..., half:]
    return jnp.concatenate([x1 * cos - x2 * sin, x2 * cos + x1 * sin], axis=-1).astype(x.dtype)


def rope_partial(x, pos, rd):
    return jnp.concatenate([rope(x[..., :rd], pos), x[..., rd:]], axis=-1)


def _qblock(t, blk):
    return t if t <= blk else math.gcd(t, blk)


def _split_blocks(x, nb):
    b, t = x.shape[:2]
    return jnp.moveaxis(x.reshape((b, nb, t // nb) + x.shape[2:]), 1, 0)


def _merge_blocks(y):
    y = jnp.moveaxis(y, 0, 1)
    return y.reshape((y.shape[0], -1) + y.shape[3:])


def dsa_attention(q, k, v, q_idx, w_idx, k_idx, q_pos, topk):
    B, T, H, d = q.shape
    L, KVH = k.shape[1], k.shape[2]
    G = H // KVH
    qb = _qblock(T, ATTN_QBLOCK)
    nb = T // qb
    k_pos = jnp.arange(L, dtype=jnp.int32)
    kif = k_idx.astype(jnp.float32)
    bi = jnp.arange(B)[:, None, None]

    def one_block(args):
        q_b, qi_b, wi_b, pos_b = args
        s = jnp.einsum('bqhe,bse->bhqs', qi_b.astype(jnp.float32), kif) * IDX_DIM ** -0.5
        score = jnp.einsum('bqh,bhqs->bqs', wi_b.astype(jnp.float32), jax.nn.relu(s))
        causal = k_pos[None, :] <= pos_b[:, None]
        score = jnp.where(causal[None], score, -jnp.inf)
        _, sel = lax.top_k(score, topk)
        valid = sel <= pos_b[None, :, None]
        ks = k[bi, sel].astype(jnp.float32)
        vs = v[bi, sel].astype(jnp.float32)
        qg = q_b.reshape(B, qb, KVH, G, d).astype(jnp.float32)
        sc = jnp.einsum('bqkgd,bqskd->bqkgs', qg, ks) * d ** -0.5
        sc = jnp.where(valid[:, :, None, None, :], sc, -jnp.inf)
        p = jax.nn.softmax(sc, axis=-1)
        o = jnp.einsum('bqkgs,bqskd->bqkgd', p, vs)
        return o.reshape(B, qb, H, d).astype(q.dtype)

    out = lax.map(one_block, (_split_blocks(q, nb), _split_blocks(q_idx, nb),
                              _split_blocks(w_idx, nb), q_pos.reshape(nb, qb)))
    return _merge_blocks(out)


def moba_attention(q, k, v, q_pos):
    B, T, H, d = q.shape
    L, KVH = k.shape[1], k.shape[2]
    G = H // KVH
    NB = -(-L // MOBA_BLOCK)
    pad = NB * MOBA_BLOCK - L
    kb = jnp.pad(k, ((0, 0), (0, pad), (0, 0), (0, 0))).reshape(B, NB, MOBA_BLOCK, KVH, d)
    vb = jnp.pad(v, ((0, 0), (0, pad), (0, 0), (0, 0))).reshape(B, NB, MOBA_BLOCK, KVH, d)
    k_mean = jnp.mean(kb.astype(jnp.float32), axis=2)
    kbt = jnp.moveaxis(kb, 3, 1)
    vbt = jnp.moveaxis(vb, 3, 1)
    ksel = min(MOBA_TOPK, (L - 1) // MOBA_BLOCK)
    qb = _qblock(T, MOBA_QBLOCK)
    nb = T // qb
    bi = jnp.arange(B)[:, None, None, None, None]
    ki = jnp.arange(KVH)[None, None, :, None, None]
    offs = jnp.arange(MOBA_BLOCK, dtype=jnp.int32)

    def one_block(args):
        q_b, pos_b = args
        own_b = pos_b // MOBA_BLOCK
        qg = q_b.reshape(B, qb, KVH, G, d).astype(jnp.float32)
        own = jnp.broadcast_to(own_b[None, :, None, None, None], (B, qb, KVH, G, 1))
        if ksel > 0:
            gate = jnp.einsum('bqkgd,bnkd->bqkgn', qg, k_mean)
            past = jnp.arange(NB)[None, :] < own_b[:, None]
            gate = jnp.where(past[None, :, None, None, :], gate, -jnp.inf)
            _, top = lax.top_k(gate, ksel)
            sel = jnp.concatenate([top, own], axis=-1)
            sel_valid = jnp.concatenate([top < own, jnp.ones_like(own, dtype=bool)], axis=-1)
        else:
            sel = own
            sel_valid = jnp.ones_like(own, dtype=bool)
        ks = kbt[bi, ki, sel].astype(jnp.float32)
        vs = vbt[bi, ki, sel].astype(jnp.float32)
        sc = jnp.einsum('bqkgd,bqkgnsd->bqkgns', qg, ks) * d ** -0.5
        key_pos = sel[..., None] * MOBA_BLOCK + offs
        mask = sel_valid[..., None] & (key_pos <= pos_b[None, :, None, None, None, None])
        sc = jnp.where(mask, sc, -jnp.inf)
        p = jax.nn.softmax(sc, axis=(-2, -1))
        o = jnp.einsum('bqkgns,bqkgnsd->bqkgd', p, vs)
        return o.reshape(B, qb, H, d).astype(q.dtype)

    out = lax.map(one_block, (_split_blocks(q, nb), q_pos.reshape(nb, qb)))
    return _merge_blocks(out)


def diff_attention(q, k, v, q_pos, lam_val):
    B, T, H = q.shape[:3]
    dh = q.shape[-1]
    L, KVH = k.shape[1], k.shape[2]
    G = H // KVH
    dv = v.shape[-1]
    qb = _qblock(T, ATTN_QBLOCK)
    nb = T // qb
    k_pos = jnp.arange(L, dtype=jnp.int32)
    kf = k.astype(jnp.float32)
    vf = v.astype(jnp.float32)
    lam_f = lam_val.astype(jnp.float32)

    def one_block(args):
        q_b, pos_b = args
        qg = q_b.reshape(B, qb, KVH, G, 2, dh).astype(jnp.float32)
        sc = jnp.einsum('bqkgcd,bskcd->bkgcqs', qg, kf) * dh ** -0.5
        causal = k_pos[None, :] <= pos_b[:, None]
        sc = jnp.where(causal, sc, -jnp.inf)
        p = jax.nn.softmax(sc, axis=-1)
        a = p[:, :, :, 0] - lam_f * p[:, :, :, 1]
        o = jnp.einsum('bkgqs,bskd->bqkgd', a, vf)
        return o.reshape(B, qb, H, dv).astype(q.dtype)

    out = lax.map(one_block, (_split_blocks(q, nb), q_pos.reshape(nb, qb)))
    return _merge_blocks(out)


def token_mixer(h, k_past, v_past, kidx_past, layer_idx, w_in, g_qk_a, g_qk_b, g_qk_c,
                g_idx_k, lam, g_sub, w_o):
    B, T, _ = h.shape
    past = k_past.shape[1]
    L = past + T
    pos = past + jnp.arange(T, dtype=jnp.int32)
    q_a, q_b, q_c, k_all, v_all, q_idx, w_idx, k_idx = jnp.split(h @ w_in, IN_SPLITS, axis=-1)
    q_a = rope(rms_norm(q_a.reshape(B, T, HEADS_A, HEAD_DIM), g_qk_a[0]), pos)
    q_b = rope(rms_norm(q_b.reshape(B, T, HEADS_B, HEAD_DIM), g_qk_b[0]), pos)
    q_c = rope(rms_norm(q_c.reshape(B, T, HEADS_C, 2, HALF_DIM), g_qk_c[0]), pos)
    k_all = k_all.reshape(B, T, KV_HEADS, HEAD_DIM)
    k_a = rope(rms_norm(k_all[:, :, :KVH_A], g_qk_a[1]), pos)
    k_b = rope(rms_norm(k_all[:, :, KVH_A:KVH_A + KVH_B], g_qk_b[1]), pos)
    k_c = rope(rms_norm(k_all[:, :, KVH_A + KVH_B:].reshape(B, T, KVH_C, 2, HALF_DIM), g_qk_c[1]), pos)
    k_new = jnp.concatenate([k_a, k_b, k_c.reshape(B, T, KVH_C, HEAD_DIM)], axis=2)
    v_new = v_all.reshape(B, T, KV_HEADS, HEAD_DIM)
    q_idx = rope_partial(q_idx.reshape(B, T, IDX_HEADS, IDX_DIM), pos, IDX_ROPE_DIM)
    kidx_new = rope_partial(rms_norm(k_idx, g_idx_k), pos, IDX_ROPE_DIM)
    w_idx = w_idx * IDX_HEADS ** -0.5

    k_full = jnp.concatenate([k_past, k_new], axis=1)
    v_full = jnp.concatenate([v_past, v_new], axis=1)
    kidx_full = jnp.concatenate([kidx_past, kidx_new], axis=1)

    o_a = dsa_attention(q_a, k_full[:, :, :KVH_A], v_full[:, :, :KVH_A], q_idx, w_idx,
                        kidx_full, pos, min(DSA_TOPK, L // 4))
    o_b = moba_attention(q_b, k_full[:, :, KVH_A:KVH_A + KVH_B], v_full[:, :, KVH_A:KVH_A + KVH_B], pos)
    lam_init = 0.8 - 0.6 * math.exp(-0.3 * layer_idx)
    lam_val = (jnp.exp(jnp.sum(lam[0].astype(jnp.float32) * lam[1].astype(jnp.float32)))
               - jnp.exp(jnp.sum(lam[2].astype(jnp.float32) * lam[3].astype(jnp.float32))) + lam_init)
    o_c = diff_attention(q_c, k_full[:, :, KVH_A + KVH_B:].reshape(B, L, KVH_C, 2, HALF_DIM),
                         v_full[:, :, KVH_A + KVH_B:], pos, lam_val)
    o_c = rms_norm(o_c, g_sub) * (1.0 - lam_init)
    o = jnp.concatenate([o_a.reshape(B, T, -1), o_b.reshape(B, T, -1), o_c.reshape(B, T, -1)], axis=-1)
    return o @ w_o, k_new, v_new, kidx_new


def memory_kv(mem, g_mem, w_ckv, g_k):
    B, M, _ = mem.shape
    mk, mv = jnp.split(rms_norm(mem, g_mem) @ w_ckv, 2, axis=-1)
    mk = rms_norm(mk.reshape(B, M, MEM_HEADS, HEAD_DIM), g_k)
    return mk, mv.reshape(B, M, MEM_HEADS, HEAD_DIM)


def memory_cross_attention(h, mem_k, mem_v, w_cq, w_co, g_q):
    B, T, _ = h.shape
    q = rms_norm((h @ w_cq).reshape(B, T, MEM_HEADS, HEAD_DIM), g_q)
    sc = jnp.einsum('bthd,bmhd->bhtm', q.astype(jnp.float32), mem_k.astype(jnp.float32)) * HEAD_DIM ** -0.5
    p = jax.nn.softmax(sc, axis=-1)
    o = jnp.einsum('bhtm,bmhd->bthd', p, mem_v.astype(jnp.float32)).astype(h.dtype)
    return o.reshape(B, T, MEM_HEADS * HEAD_DIM) @ w_co


def decoder_layer(x, k_past, v_past, kidx_past, mem_k, mem_v, layer_idx, w_in, g_qk_a, g_qk_b,
                  g_qk_c, g_idx_k, lam, g_sub, w_o, g_mix, g_cross, w_cq, w_co, g_q_x, g_mlp,
                  w_up, w_down):
    mix, k_new, v_new, kidx_new = token_mixer(rms_norm(x, g_mix), k_past, v_past, kidx_past,
                                              layer_idx, w_in, g_qk_a, g_qk_b, g_qk_c, g_idx_k,
                                              lam, g_sub, w_o)
    x = x + mix
    x = x + memory_cross_attention(rms_norm(x, g_cross), mem_k, mem_v, w_cq, w_co, g_q_x)
    h = rms_norm(x, g_mlp)
    x = x + jnp.square(jax.nn.relu(h @ w_up)) @ w_down
    return x, k_new, v_new, kidx_new


def setup_inputs(seed: int = 0) -> dict:
    key = jax.random.key(seed)
    ks = jax.random.split(key, 28)
    f32 = jnp.float32
    n_pages = PAST_LEN // PAGE_SIZE
    n_phys = (5 * DEC_BATCH * n_pages + 3) // 4

    def normal(k, shape, scale=1.0):
        return jax.random.normal(k, shape, f32) * scale

    def gain(k, shape):
        return 1.0 + 0.02 * jax.random.normal(k, shape, f32)

    page_table = jax.random.permutation(ks[8], n_phys)[: DEC_BATCH * n_pages]
    page_table = page_table.reshape(DEC_BATCH, n_pages).astype(jnp.int32)
    mix_w = MIX_HEADS * HEAD_DIM
    mem_w = MEM_HEADS * HEAD_DIM
    return {
        'x_prompt': normal(ks[0], (BATCH, SEQ, D_MODEL)),
        'x_sample': normal(ks[1], (DEC_BATCH, DEC_SEQ, D_MODEL)),
        'mem_prompt': normal(ks[2], (BATCH, MEM_TOKENS, D_MODEL)),
        'cache_k': normal(ks[3], (DEPTH, n_phys, PAGE_SIZE, KV_HEADS, HEAD_DIM)),
        'cache_v': normal(ks[4], (DEPTH, n_phys, PAGE_SIZE, KV_HEADS, HEAD_DIM)),
        'cache_k_idx': normal(ks[5], (DEPTH, n_phys, PAGE_SIZE, IDX_DIM)),
        'cache_mem_k': normal(ks[6], (DEPTH, DEC_BATCH, MEM_TOKENS, MEM_HEADS, HEAD_DIM)),
        'cache_mem_v': normal(ks[7], (DEPTH, DEC_BATCH, MEM_TOKENS, MEM_HEADS, HEAD_DIM)),
        'page_table': page_table,
        'w_in': normal(ks[9], (DEPTH, D_MODEL, IN_DIM), D_MODEL ** -0.5),
        'g_qk_a': gain(ks[10], (DEPTH, 2, HEAD_DIM)),
        'g_qk_b': gain(ks[11], (DEPTH, 2, HEAD_DIM)),
        'g_qk_c': gain(ks[12], (DEPTH, 2, HALF_DIM)),
        'g_idx_k': gain(ks[13], (DEPTH, IDX_DIM)),
        'lam': normal(ks[14], (DEPTH, 4, HALF_DIM), 0.1),
        'g_sub': gain(ks[15], (DEPTH, HEAD_DIM)),
        'w_o': normal(ks[16], (DEPTH, mix_w, D_MODEL), mix_w ** -0.5),
        'g_mix': gain(ks[17], (DEPTH, D_MODEL)),
        'g_cross': gain(ks[18], (DEPTH, D_MODEL)),
        'g_mem': gain(ks[19], (DEPTH, D_MODEL)),
        'w_cq': normal(ks[20], (DEPTH, D_MODEL, mem_w), D_MODEL ** -0.5),
        'w_ckv': normal(ks[21], (DEPTH, D_MODEL, 2 * mem_w), D_MODEL ** -0.5),
        'w_co': normal(ks[22], (DEPTH, mem_w, D_MODEL), mem_w ** -0.5),
        'g_qk_x': gain(ks[23], (DEPTH, 2, HEAD_DIM)),
        'g_mlp': gain(ks[24], (DEPTH, D_MODEL)),
        'w_up': normal(ks[25], (DEPTH, D_MODEL, D_FF), D_MODEL ** -0.5),
        'w_down': normal(ks[26], (DEPTH, D_FF, D_MODEL), D_FF ** -0.5),
    }


def reference(x_prompt, x_sample, mem_prompt, cache_k, cache_v, cache_k_idx, cache_mem_k,
              cache_mem_v, page_table, w_in, g_qk_a, g_qk_b, g_qk_c, g_idx_k, lam, g_sub, w_o,
              g_mix, g_cross, g_mem, w_cq, w_ckv, w_co, g_qk_x, g_mlp, w_up, w_down):
    B = x_prompt.shape[0]
    Bd = x_sample.shape[0]
    n_pages = page_table.shape[1]
    past_len = n_pages * cache_k.shape[2]
    xp, xs = x_prompt, x_sample
    kp_l, vp_l, ip_l, mkp_l, mvp_l, ks_l, vs_l, is_l = [], [], [], [], [], [], [], []
    for l in range(DEPTH):
        lw = (w_in[l], g_qk_a[l], g_qk_b[l], g_qk_c[l], g_idx_k[l], lam[l], g_sub[l], w_o[l],
              g_mix[l], g_cross[l], w_cq[l], w_co[l], g_qk_x[l, 0], g_mlp[l], w_up[l], w_down[l])
        empty_kv = jnp.zeros((B, 0, KV_HEADS, HEAD_DIM), xp.dtype)
        empty_idx = jnp.zeros((B, 0, IDX_DIM), xp.dtype)
        mk_p, mv_p = memory_kv(mem_prompt, g_mem[l], w_ckv[l], g_qk_x[l, 1])
        xp, k_p, v_p, i_p = decoder_layer(xp, empty_kv, empty_kv, empty_idx, mk_p, mv_p, l, *lw)
        k_past = cache_k[l][page_table].reshape(Bd, past_len, KV_HEADS, HEAD_DIM)
        v_past = cache_v[l][page_table].reshape(Bd, past_len, KV_HEADS, HEAD_DIM)
        i_past = cache_k_idx[l][page_table].reshape(Bd, past_len, IDX_DIM)
        xs, k_s, v_s, i_s = decoder_layer(xs, k_past, v_past, i_past, cache_mem_k[l],
                                          cache_mem_v[l], l, *lw)
        kp_l.append(k_p); vp_l.append(v_p); ip_l.append(i_p)
        mkp_l.append(mk_p); mvp_l.append(mv_p)
        ks_l.append(k_s); vs_l.append(v_s); is_l.append(i_s)
    return (xp, xs, jnp.stack(kp_l), jnp.stack(vp_l), jnp.stack(ip_l), jnp.stack(mkp_l),
            jnp.stack(mvp_l), jnp.stack(ks_l), jnp.stack(vs_l), jnp.stack(is_l))
```

```python
import functools
import math

import jax
import jax.numpy as jnp
from jax import lax
from jax.experimental import pallas as pl
from jax.experimental.pallas import tpu as pltpu

F32 = jnp.float32
BF16 = jnp.bfloat16
I32 = jnp.int32

HEAD_DIM = 128
HALF_DIM = 64
HEADS_A, HEADS_B, HEADS_C = 12, 12, 8
KVH = 4
GA, GB, GC = 3, 3, 2
IDX_HEADS = 32
IDX_DIM = 128
DSA_TOPK = 256
MOBA_BLOCK = 256
MOBA_SHIFT = 8
MOBA_TOPK = 3
MEM_HEADS = 4
ROPE_THETA = 10000.0
NORM_EPS = 1e-6
LANE = 128
TN = 512
VMEM_LIMIT = 56 * 1024 * 1024
INT_MIN = -(2 ** 31)
NEG_INF = float("-inf")

C_QA, C_QB, C_QC, C_QI, C_K, C_V, C_KIDX, C_WIDX = 0, 12, 24, 32, 64, 76, 88, 89
N_CHUNKS = 92
N_NONF32_TILES = 16
F_K, F_V, F_KIDX, F_WIDX = 0, 12, 24, 25


def _cparams(*sem):
    return pltpu.CompilerParams(dimension_semantics=sem, vmem_limit_bytes=VMEM_LIMIT)


def _dot_nt(a, b):
    return lax.dot_general(a, b, (((1,), (1,)), ((), ())), preferred_element_type=F32)


def _dot(a, b):
    return jnp.dot(a, b, preferred_element_type=F32)


def _proj_kernel(meta_ref, x_ref, gn_ref, w_ref, gains_ref, tab_ref, *rest, n_nonf32, has_f32):
    if has_f32:
        obf_ref, of32_ref, h_scr = rest
    else:
        obf_ref, h_scr = rest
        of32_ref = None
    j = pl.program_id(1)

    @pl.when(j == 0)
    def _():
        x = x_ref[...]
        ms = jnp.mean(x * x, axis=-1, keepdims=True)
        h_scr[...] = (x * lax.rsqrt(ms + NORM_EPS) * gn_ref[...]).astype(BF16)

    acc = _dot(h_scr[...], w_ref[...])
    lane = lax.broadcasted_iota(I32, (1, LANE), 1)
    lo = lane < HALF_DIM
    for c in range(TN // LANE):
        ci = j * (TN // LANE) + c
        mode = meta_ref[0, ci]
        gi = meta_ref[1, ci]
        ti = meta_ref[2, ci]
        xx = acc[:, c * LANE:(c + 1) * LANE]
        ss = xx * xx
        s_full = jnp.sum(ss, axis=-1, keepdims=True)
        s_lo = jnp.sum(jnp.where(lo, ss, 0.0), axis=-1, keepdims=True)
        s_hi = jnp.sum(jnp.where(lo, 0.0, ss), axis=-1, keepdims=True)
        ms = jnp.where(mode == 1, s_full * (1.0 / HEAD_DIM), jnp.where(lo, s_lo, s_hi) * (1.0 / HALF_DIM))
        inv = jnp.where(mode == 0, 1.0, lax.rsqrt(ms + NORM_EPS))
        y = xx * inv * gains_ref[pl.ds(gi, 1), :]
        out = (y * tab_ref[ti, 0]
               + pltpu.roll(y, 64, 1) * tab_ref[ti, 1]
               + pltpu.roll(y, 96, 1) * tab_ref[ti, 2]
               + pltpu.roll(y, 32, 1) * tab_ref[ti, 3])
        obf_ref[:, c * LANE:(c + 1) * LANE] = out.astype(BF16)
        if has_f32:
            @pl.when(j >= n_nonf32)
            def _():
                of32_ref[:, c * LANE:(c + 1) * LANE] = out


def _proj(x, g_norm, w_bf, meta, gains, tabs, *, tm, n_nonf32, n_pos_tiles, name):
    m, k = x.shape
    n = w_bf.shape[1]
    nt = n // TN
    nf32 = nt - n_nonf32
    ntype = tabs.shape[0]
    in_specs = [
        pl.BlockSpec((tm, k), lambda i, j, mt: (i, 0)),
        pl.BlockSpec((1, k), lambda i, j, mt: (0, 0)),
        pl.BlockSpec((k, TN), lambda i, j, mt: (0, j)),
        pl.BlockSpec(gains.shape, lambda i, j, mt: (0, 0)),
        pl.BlockSpec((ntype, 4, tm, LANE), lambda i, j, mt: (0, 0, lax.rem(i, n_pos_tiles), 0)),
    ]
    out_shape = [jax.ShapeDtypeStruct((m, n), BF16)]
    out_specs = [pl.BlockSpec((tm, TN), lambda i, j, mt: (i, j))]
    if nf32:
        out_shape.append(jax.ShapeDtypeStruct((m, nf32 * TN), F32))
        out_specs.append(pl.BlockSpec((tm, TN), lambda i, j, mt: (i, jnp.maximum(j - n_nonf32, 0))))
    res = pl.pallas_call(
        functools.partial(_proj_kernel, n_nonf32=n_nonf32, has_f32=bool(nf32)),
        grid_spec=pltpu.PrefetchScalarGridSpec(
            num_scalar_prefetch=1, grid=(m // tm, nt), in_specs=in_specs, out_specs=out_specs,
            scratch_shapes=[pltpu.VMEM((tm, k), BF16)]),
        out_shape=out_shape,
        compiler_params=_cparams("parallel", "arbitrary"),
        name=name,
    )(meta, x, g_norm.reshape(1, k), w_bf, gains, tabs)
    return res if nf32 else (res[0], None)


def _resmm_kernel(*refs, n_pairs):
    res_ref = refs[0]
    a_refs = refs[1:1 + n_pairs]
    w_refs = refs[1 + n_pairs:1 + 2 * n_pairs]
    o_ref = refs[1 + 2 * n_pairs]
    acc = res_ref[...]
    for a_ref, w_ref in zip(a_refs, w_refs):
        acc = acc + _dot(a_ref[...], w_ref[...])
    o_ref[...] = acc


def _resmm(res, a_list, w_list, *, tm, tn, name):
    m, n = res.shape
    in_specs = [pl.BlockSpec((tm, tn), lambda i, j: (i, j))]
    in_specs += [pl.BlockSpec((tm, a.shape[1]), lambda i, j: (i, 0)) for a in a_list]
    in_specs += [pl.BlockSpec((w.shape[0], tn), lambda i, j: (0, j)) for w in w_list]
    return pl.pallas_call(
        functools.partial(_resmm_kernel, n_pairs=len(a_list)),
        grid=(m // tm, n // tn),
        in_specs=in_specs,
        out_specs=pl.BlockSpec((tm, tn), lambda i, j: (i, j)),
        out_shape=jax.ShapeDtypeStruct((m, n), F32),
        compiler_params=_cparams("parallel", "arbitrary"),
        name=name,
    )(res, *a_list, *w_list)


def _mlp_kernel(x_ref, g_ref, wu_ref, wd_ref, o_ref, h_scr):
    f = pl.program_id(1)

    @pl.when(f == 0)
    def _():
        x = x_ref[...]
        ms = jnp.mean(x * x, axis=-1, keepdims=True)
        h_scr[...] = (x * lax.rsqrt(ms + NORM_EPS) * g_ref[...]).astype(BF16)
        o_ref[...] = x

    u = _dot(h_scr[...], wu_ref[...])
    u = jnp.square(jnp.maximum(u, 0.0)).astype(BF16)
    o_ref[...] += _dot(u, wd_ref[...])


def _mlp(x, g, wu_bf, wd_bf, *, tm, tf, name):
    m, d = x.shape
    ff = wu_bf.shape[1]
    return pl.pallas_call(
        _mlp_kernel,
        grid=(m // tm, ff // tf),
        in_specs=[
            pl.BlockSpec((tm, d), lambda i, f: (i, 0)),
            pl.BlockSpec((1, d), lambda i, f: (0, 0)),
            pl.BlockSpec((d, tf), lambda i, f: (0, f)),
            pl.BlockSpec((tf, d), lambda i, f: (f, 0)),
        ],
        out_specs=pl.BlockSpec((tm, d), lambda i, f: (i, 0)),
        out_shape=jax.ShapeDtypeStruct((m, d), F32),
        scratch_shapes=[pltpu.VMEM((tm, d), BF16)],
        compiler_params=_cparams("parallel", "arbitrary"),
        name=name,
    )(x, g.reshape(1, d), wu_bf, wd_bf)


def _order_keys(score):
    bits = lax.bitcast_convert_type(score, I32)
    return jnp.where(bits < 0, bits ^ jnp.int32(0x7FFFFFFF), bits)


def _kth_largest_key(key_ref, rows, k):
    def body(it, thr):
        bit = jnp.left_shift(jnp.int32(1), jnp.int32(31) - it)
        cand = thr ^ bit
        cnt = jnp.sum(jnp.where(key_ref[...] >= cand, 1.0, 0.0), axis=-1, keepdims=True)
        return jnp.where(cnt >= k, cand, thr)
    return lax.fori_loop(0, 32, body, jnp.full((rows, 1), INT_MIN, I32))


def _softmax_pv(sc, v_bf):
    mx = jnp.max(sc, axis=-1, keepdims=True)
    p = jnp.exp(sc - mx)
    l = jnp.sum(p, axis=-1, keepdims=True)
    return _dot(p.astype(BF16), v_bf) / l


def _moba_select(gate, past, ksel):
    lane = lax.broadcasted_iota(I32, gate.shape, 1).astype(F32)
    g = jnp.where(past, gate, NEG_INF)
    sel = jnp.zeros(gate.shape, jnp.bool_)
    for _ in range(ksel):
        mx = jnp.max(g, axis=-1, keepdims=True)
        idx = jnp.min(jnp.where(g == mx, lane, float(LANE)), axis=-1, keepdims=True)
        hit = (lane == idx) & past
        sel = sel | hit
        g = jnp.where(lane == idx, NEG_INF, g)
    return sel


def _lam_value(lam_ref, lam_init):
    lam = lam_ref[...]
    s1 = jnp.sum(lam[0:1, :] * lam[1:2, :], axis=-1, keepdims=True)
    s2 = jnp.sum(lam[2:3, :] * lam[3:4, :], axis=-1, keepdims=True)
    return jnp.exp(s1) - jnp.exp(s2) + lam_init


def _sub_norm(o, gsub_ref, lam_init):
    ms = jnp.mean(o * o, axis=-1, keepdims=True)
    return (o * lax.rsqrt(ms + NORM_EPS) * gsub_ref[...]) * (1.0 - lam_init)


TQ = 128


def _dsa_prompt_kernel(qa_ref, qi_ref, kidx_ref, k_ref, v_ref, w_ref, o_ref, key_scr, sel_scr, *, topk, t_len):
    i = pl.program_id(1)
    kidx = kidx_ref[...]
    w = w_ref[...] * (IDX_HEADS ** -0.5)
    score = jnp.zeros((TQ, t_len), F32)
    for h in range(IDX_HEADS):
        s = _dot_nt(qi_ref[:, h * LANE:(h + 1) * LANE], kidx) * (IDX_DIM ** -0.5)
        score = score + w[:, h:h + 1] * jnp.maximum(s, 0.0)
    qpos = i * TQ + lax.broadcasted_iota(I32, (TQ, 1), 0)
    kpos = lax.broadcasted_iota(I32, (1, t_len), 1)
    causal = kpos <= qpos
    key_scr[...] = _order_keys(jnp.where(causal, score, NEG_INF))
    thr = _kth_largest_key(key_scr, TQ, topk)

    keys = key_scr[...]
    gt = keys > thr
    eq = keys == thr
    need = topk - jnp.sum(jnp.where(gt, 1.0, 0.0), axis=-1, keepdims=True)
    cw = 256
    tri = (lax.broadcasted_iota(I32, (cw, cw), 0) <= lax.broadcasted_iota(I32, (cw, cw), 1)).astype(BF16)
    carry = jnp.zeros((TQ, 1), F32)
    for c in range(t_len // cw):
        sl = slice(c * cw, (c + 1) * cw)
        eq_c = jnp.where(eq[:, sl], 1.0, 0.0)
        pre = _dot(eq_c.astype(BF16), tri) + carry
        keep = gt[:, sl] | (eq[:, sl] & (pre <= need))
        sel_scr[:, sl] = jnp.where(keep & causal[:, sl], 1.0, 0.0)
        carry = carry + jnp.sum(eq_c, axis=-1, keepdims=True)

    sel = jnp.concatenate([sel_scr[...]] * GA, axis=0) > 0.5
    for kv in range(KVH):
        q3 = jnp.concatenate([qa_ref[:, (kv * GA + g) * LANE:(kv * GA + g + 1) * LANE] for g in range(GA)], axis=0)
        sc = _dot_nt(q3, k_ref[:, kv * LANE:(kv + 1) * LANE]) * (HEAD_DIM ** -0.5)
        o = _softmax_pv(jnp.where(sel, sc, NEG_INF), v_ref[:, kv * LANE:(kv + 1) * LANE])
        for g in range(GA):
            o_ref[:, (kv * GA + g) * LANE:(kv * GA + g + 1) * LANE] = o[g * TQ:(g + 1) * TQ].astype(o_ref.dtype)


def _moba_prompt_kernel(qb_ref, k_ref, v_ref, kf_ref, o_ref, kmean_scr, *, t_len, ksel):
    i = pl.program_id(1)
    nb = t_len // MOBA_BLOCK

    @pl.when(i == 0)
    def _():
        kmean_scr[...] = jnp.zeros_like(kmean_scr)
        for n in range(nb):
            kmean_scr[n:n + 1, :] = jnp.mean(kf_ref[n * MOBA_BLOCK:(n + 1) * MOBA_BLOCK, :], axis=0, keepdims=True)

    rows = GB * TQ
    qpos = i * TQ + (lax.broadcasted_iota(I32, (rows, 1), 0) & (TQ - 1))
    own = qpos >> MOBA_SHIFT
    blk = lax.broadcasted_iota(I32, (rows, LANE), 1)
    kpos = lax.broadcasted_iota(I32, (1, t_len), 1)
    own_causal = ((kpos >> MOBA_SHIFT) == own) & (kpos <= qpos)
    expand = jnp.where((lax.broadcasted_iota(I32, (LANE, t_len), 1) >> MOBA_SHIFT)
                       == lax.broadcasted_iota(I32, (LANE, t_len), 0), 1.0, 0.0).astype(BF16)
    for kv in range(KVH):
        q3 = jnp.concatenate([qb_ref[:, (kv * GB + g) * LANE:(kv * GB + g + 1) * LANE] for g in range(GB)], axis=0)
        mask = own_causal
        if ksel > 0:
            gate = _dot_nt(q3, kmean_scr[:, kv * LANE:(kv + 1) * LANE].astype(BF16))
            sel = _moba_select(gate, blk < own, ksel)
            mask = mask | (_dot(jnp.where(sel, 1.0, 0.0).astype(BF16), expand) > 0.5)
        sc = _dot_nt(q3, k_ref[:, kv * LANE:(kv + 1) * LANE]) * (HEAD_DIM ** -0.5)
        o = _softmax_pv(jnp.where(mask, sc, NEG_INF), v_ref[:, kv * LANE:(kv + 1) * LANE])
        for g in range(GB):
            o_ref[:, (kv * GB + g) * LANE:(kv * GB + g + 1) * LANE] = o[g * TQ:(g + 1) * TQ].astype(o_ref.dtype)


def _diff_prompt_kernel(qc_ref, k_ref, v_ref, lam_ref, gsub_ref, o_ref, *, t_len, lam_init):
    i = pl.program_id(1)
    lam_val = _lam_value(lam_ref, lam_init)
    rows = 2 * GC * TQ
    qpos = i * TQ + (lax.broadcasted_iota(I32, (rows, 1), 0) & (TQ - 1))
    kpos = lax.broadcasted_iota(I32, (1, t_len), 1)
    causal = kpos <= qpos
    lo = lax.broadcasted_iota(I32, (1, LANE), 1) < HALF_DIM
    zero = jnp.zeros((), BF16)
    for kv in range(KVH):
        qs = [qc_ref[:, (kv * GC + g) * LANE:(kv * GC + g + 1) * LANE] for g in range(GC)]
        q4 = jnp.concatenate([jnp.where(lo, q, zero) for q in qs] + [jnp.where(lo, zero, q) for q in qs], axis=0)
        sc = _dot_nt(q4, k_ref[:, kv * LANE:(kv + 1) * LANE]) * (HALF_DIM ** -0.5)
        sc = jnp.where(causal, sc, NEG_INF)
        mx = jnp.max(sc, axis=-1, keepdims=True)
        p = jnp.exp(sc - mx)
        r = 1.0 / jnp.sum(p, axis=-1, keepdims=True)
        a = p[:GC * TQ] * r[:GC * TQ] - p[GC * TQ:] * (lam_val * r[GC * TQ:])
        o = _sub_norm(_dot(a.astype(BF16), v_ref[:, kv * LANE:(kv + 1) * LANE]), gsub_ref, lam_init)
        for g in range(GC):
            o_ref[:, (kv * GC + g) * LANE:(kv * GC + g + 1) * LANE] = o[g * TQ:(g + 1) * TQ].astype(o_ref.dtype)


def _prompt_mixers(post, pf32, lam, g_sub, lam_init, *, batch, t_len):
    m = post.shape[0]
    nq = t_len // TQ
    qrow = lambda b, i: b * nq + i
    topk = min(DSA_TOPK, t_len // 4)
    ksel = min(MOBA_TOPK, (t_len - 1) // MOBA_BLOCK)
    o_a = pl.pallas_call(
        functools.partial(_dsa_prompt_kernel, topk=topk, t_len=t_len),
        grid=(batch, nq),
        in_specs=[
            pl.BlockSpec((TQ, HEADS_A * LANE), lambda b, i: (qrow(b, i), C_QA // HEADS_A)),
            pl.BlockSpec((TQ, IDX_HEADS * LANE), lambda b, i: (qrow(b, i), C_QI // IDX_HEADS)),
            pl.BlockSpec((t_len, LANE), lambda b, i: (b, C_KIDX)),
            pl.BlockSpec((t_len, KVH * LANE), lambda b, i: (b, C_K // KVH)),
            pl.BlockSpec((t_len, KVH * LANE), lambda b, i: (b, C_V // KVH)),
            pl.BlockSpec((TQ, LANE), lambda b, i: (qrow(b, i), F_WIDX)),
        ],
        out_specs=pl.BlockSpec((TQ, HEADS_A * LANE), lambda b, i: (qrow(b, i), 0)),
        out_shape=jax.ShapeDtypeStruct((m, HEADS_A * LANE), BF16),
        scratch_shapes=[pltpu.VMEM((TQ, t_len), I32), pltpu.VMEM((TQ, t_len), F32)],
        compiler_params=_cparams("parallel", "arbitrary"),
        name="dsa_prompt",
    )(post, post, post, post, post, pf32)
    o_b = pl.pallas_call(
        functools.partial(_moba_prompt_kernel, t_len=t_len, ksel=ksel),
        grid=(batch, nq),
        in_specs=[
            pl.BlockSpec((TQ, HEADS_B * LANE), lambda b, i: (qrow(b, i), C_QB // HEADS_B)),
            pl.BlockSpec((t_len, KVH * LANE), lambda b, i: (b, C_K // KVH + 1)),
            pl.BlockSpec((t_len, KVH * LANE), lambda b, i: (b, C_V // KVH + 1)),
            pl.BlockSpec((t_len, KVH * LANE), lambda b, i: (b, F_K // KVH + 1)),
        ],
        out_specs=pl.BlockSpec((TQ, HEADS_B * LANE), lambda b, i: (qrow(b, i), 0)),
        out_shape=jax.ShapeDtypeStruct((m, HEADS_B * LANE), BF16),
        scratch_shapes=[pltpu.VMEM((LANE, KVH * LANE), F32)],
        compiler_params=_cparams("parallel", "arbitrary"),
        name="moba_prompt",
    )(post, post, post, pf32)
    o_c = pl.pallas_call(
        functools.partial(_diff_prompt_kernel, t_len=t_len, lam_init=lam_init),
        grid=(batch, nq),
        in_specs=[
            pl.BlockSpec((TQ, HEADS_C * LANE), lambda b, i: (qrow(b, i), C_QC // HEADS_C)),
            pl.BlockSpec((t_len, KVH * LANE), lambda b, i: (b, C_K // KVH + 2)),
            pl.BlockSpec((t_len, KVH * LANE), lambda b, i: (b, C_V // KVH + 2)),
            pl.BlockSpec(lam.shape, lambda b, i: (0, 0)),
            pl.BlockSpec((1, LANE), lambda b, i: (0, 0)),
        ],
        out_specs=pl.BlockSpec((TQ, HEADS_C * LANE), lambda b, i: (qrow(b, i), 0)),
        out_shape=jax.ShapeDtypeStruct((m, HEADS_C * LANE), BF16),
        compiler_params=_cparams("parallel", "arbitrary"),
        name="diff_prompt",
    )(post, post, post, lam, g_sub.reshape(1, LANE))
    return o_a, o_b, o_c


def _cross_kernel(q_ref, mk_ref, mv_ref, o_ref):
    for h in range(MEM_HEADS):
        sl = slice(h * LANE, (h + 1) * LANE)
        sc = _dot_nt(q_ref[:, sl].astype(BF16), mk_ref[:, sl]) * (HEAD_DIM ** -0.5)
        o_ref[:, sl] = _softmax_pv(sc, mv_ref[:, sl]).astype(o_ref.dtype)


def _cross(q, mk_bf, mv_bf, *, batch, t_len, tq, out_dtype, name):
    m, w = q.shape
    nq = t_len // tq
    mem = mk_bf.shape[0] // batch
    return pl.pallas_call(
        _cross_kernel,
        grid=(batch, nq),
        in_specs=[
            pl.BlockSpec((tq, w), lambda b, i: (b * nq + i, 0)),
            pl.BlockSpec((mem, w), lambda b, i: (b, 0)),
            pl.BlockSpec((mem, w), lambda b, i: (b, 0)),
        ],
        out_specs=pl.BlockSpec((tq, w), lambda b, i: (b * nq + i, 0)),
        out_shape=jax.ShapeDtypeStruct((m, w), out_dtype),
        compiler_params=_cparams("parallel", "arbitrary"),
        name=name,
    )(q, mk_bf, mv_bf)


def _sample_index_kernel(pt_ref, page_ref, knew_ref, qi_ref, w_ref, sel_ref, key_scr, *, n_pages, page, t_new, topk):
    p = pl.program_id(1)
    width = (n_pages + 1) * page
    kp = jnp.where(p < n_pages, page_ref[0, 0], knew_ref[0]).astype(BF16)
    s = _dot_nt(qi_ref[0], kp) * (IDX_DIM ** -0.5)
    r = jnp.maximum(s, 0.0) * (w_ref[0] * (IDX_HEADS ** -0.5))
    score = jnp.sum(r.reshape(IDX_HEADS, t_new, page), axis=0)
    qpos = n_pages * page + lax.broadcasted_iota(I32, (t_new, 1), 0)
    kpos = p * page + lax.broadcasted_iota(I32, (1, page), 1)
    score = jnp.where(kpos <= qpos, score, NEG_INF)
    key_scr[:, pl.ds(pl.multiple_of(p * page, page), page)] = _order_keys(score)

    @pl.when(p == n_pages)
    def _():
        thr = _kth_largest_key(key_scr, t_new, topk)
        keys = key_scr[...]
        gt = keys > thr
        need = topk - jnp.sum(jnp.where(gt, 1.0, 0.0), axis=-1, keepdims=True)
        tri = (lax.broadcasted_iota(I32, (page, page), 0) <= lax.broadcasted_iota(I32, (page, page), 1)).astype(BF16)
        kpos_all = lax.broadcasted_iota(I32, (1, page), 1)

        def body(c, carry):
            off = pl.multiple_of(c * page, page)
            kc = key_scr[:, pl.ds(off, page)]
            eq_c = jnp.where(kc == thr, 1.0, 0.0)
            pre = _dot(eq_c.astype(BF16), tri) + carry
            keep = (kc > thr) | ((kc == thr) & (pre <= need))
            keep = keep & ((c * page + kpos_all) <= qpos)
            sel_ref[0, :, pl.ds(off, page)] = jnp.where(keep, 1.0, 0.0)
            return carry + jnp.sum(eq_c, axis=-1, keepdims=True)
        lax.fori_loop(0, width // page, body, jnp.zeros((t_new, 1), F32))


def _online_update(m_ref, l_ref, acc_ref, sc, pv_fn):
    m_old = m_ref[...]
    m_new = jnp.maximum(m_old, jnp.max(sc, axis=-1, keepdims=True))
    m_safe = jnp.where(m_new == NEG_INF, 0.0, m_new)
    alpha = jnp.exp(m_old - m_safe)
    p = jnp.exp(sc - m_safe)
    l_ref[...] = alpha * l_ref[...] + jnp.sum(p, axis=-1, keepdims=True)
    acc_ref[...] = alpha * acc_ref[...] + pv_fn(p.astype(BF16))
    m_ref[...] = m_new


def _sample_attn_kernel(pt_ref, kpage_ref, vpage_ref, knew_ref, vnew_ref, sel_ref, qa_ref, qb_ref, qc_ref,
                        lam_ref, gsub_ref, o_ref,
                        am, al, aacc, bm, bl, bacc, blk_m, blk_l, blk_acc, ksum, kmean, cm, cl, cacc,
                        *, n_pages, page, t_new, ksel, lam_init):
    p = pl.program_id(1)
    rows = KVH * 32
    ppb = MOBA_BLOCK // page
    n_blk = n_pages // ppb + 1

    @pl.when(p == 0)
    def _():
        for r in (am, bm, cm):
            r[...] = jnp.full(r.shape, NEG_INF, F32)
        for r in (al, aacc, bl, bacc, cl, cacc, ksum, kmean):
            r[...] = jnp.zeros(r.shape, F32)

    qpos = n_pages * page + (lax.broadcasted_iota(I32, (rows, 1), 0) & (t_new - 1))
    kpos = p * page + lax.broadcasted_iota(I32, (1, page), 1)
    causal = kpos <= qpos

    def step(k_head, v_head):
        def scores(q_ref, h0, scale):
            return jnp.concatenate(
                [_dot_nt(q_ref[0, kv * 32:(kv + 1) * 32, :], k_head(h0 + kv).astype(BF16))
                 for kv in range(KVH)], axis=0) * scale

        def pv(h0):
            def fn(pb):
                return jnp.concatenate(
                    [_dot(pb[kv * 32:(kv + 1) * 32], v_head(h0 + kv).astype(BF16))
                     for kv in range(KVH)], axis=0)
            return fn

        sel = jnp.concatenate([sel_ref[0, 0]] * (rows // t_new), axis=0) > 0.5
        sc = jnp.where(sel & causal, scores(qa_ref, 0, HEAD_DIM ** -0.5), NEG_INF)
        _online_update(am, al, aacc, sc, pv(0))
        sc = jnp.where(causal, scores(qb_ref, KVH, HEAD_DIM ** -0.5), NEG_INF)
        _online_update(bm, bl, bacc, sc, pv(KVH))
        for kv in range(KVH):
            ksum[:, kv * LANE:(kv + 1) * LANE] += jnp.sum(k_head(KVH + kv), axis=0, keepdims=True)
        sc = jnp.where(causal, scores(qc_ref, 2 * KVH, HALF_DIM ** -0.5), NEG_INF)
        _online_update(cm, cl, cacc, sc, pv(2 * KVH))

    @pl.when(p < n_pages)
    def _():
        step(lambda h: kpage_ref[0, 0, h], lambda h: vpage_ref[0, 0, h])

    @pl.when(p == n_pages)
    def _():
        step(lambda h: knew_ref[0, h], lambda h: vnew_ref[0, h])

    @pl.when((lax.rem(p, ppb) == ppb - 1) | (p == n_pages))
    def _():
        n = p // ppb
        blk_m[n] = bm[...]
        blk_l[n] = bl[...]
        blk_acc[n] = bacc[...]
        kmean[pl.ds(n, 1), :] = ksum[...] * (1.0 / MOBA_BLOCK)
        bm[...] = jnp.full(bm.shape, NEG_INF, F32)
        bl[...] = jnp.zeros(bl.shape, F32)
        bacc[...] = jnp.zeros(bacc.shape, F32)
        ksum[...] = jnp.zeros(ksum.shape, F32)

    @pl.when(p == n_pages)
    def _():
        o_ref[0, 0:rows, :] = aacc[...] / al[...]
        own = qpos >> MOBA_SHIFT
        blk = lax.broadcasted_iota(I32, (rows, LANE), 1)
        chosen = blk == own
        if ksel > 0:
            gate = jnp.concatenate(
                [_dot_nt(qb_ref[0, kv * 32:(kv + 1) * 32, :], kmean[:, kv * LANE:(kv + 1) * LANE].astype(BF16))
                 for kv in range(KVH)], axis=0)
            chosen = chosen | _moba_select(gate, blk < own, ksel)
        chosen_f = jnp.where(chosen, 1.0, 0.0)
        mx = jnp.full((rows, 1), NEG_INF, F32)
        for n in range(n_blk):
            mx = jnp.maximum(mx, jnp.where(chosen_f[:, n:n + 1] > 0.5, blk_m[n], NEG_INF))
        l = jnp.zeros((rows, 1), F32)
        acc = jnp.zeros((rows, LANE), F32)
        for n in range(n_blk):
            wgt = jnp.where(chosen_f[:, n:n + 1] > 0.5, jnp.exp(blk_m[n] - mx), 0.0)
            l = l + wgt * blk_l[n]
            acc = acc + wgt * blk_acc[n]
        o_ref[0, rows:2 * rows, :] = acc / l
        lam_val = _lam_value(lam_ref, lam_init)
        oc = cacc[...] / cl[...]
        half = GC * t_new
        for kv in range(KVH):
            d = oc[kv * 32:kv * 32 + half] - lam_val * oc[kv * 32 + half:(kv + 1) * 32]
            o_ref[0, 2 * rows + kv * half:2 * rows + (kv + 1) * half, :] = _sub_norm(d, gsub_ref, lam_init)


def _sample_mixers(post, pf32, cache_k, cache_v, cache_ki, layer, page_table, lam, g_sub, lam_init, *, batch, t_new):
    page = cache_k.shape[2]
    n_kv = cache_k.shape[3]
    n_pages = page_table.shape[1]
    past = n_pages * page
    assert page == LANE and MOBA_BLOCK % page == 0 and past % MOBA_BLOCK == 0 and t_new == 8
    assert MOBA_BLOCK == 1 << MOBA_SHIFT
    assert GA * t_new <= 32 and 2 * GC * t_new == 32
    seq = past + t_new
    topk = min(DSA_TOPK, seq // 4)
    ksel = min(MOBA_TOPK, (seq - 1) // MOBA_BLOCK)
    n_blk = past // MOBA_BLOCK + 1
    assert n_blk <= LANE
    width = (n_pages + 1) * page
    assert n_kv == 3 * KVH

    def new_page(x, heads):
        x = x.reshape(batch, t_new, heads, LANE).transpose(0, 2, 1, 3)
        return jnp.pad(x, ((0, 0), (0, 0), (0, page - t_new), (0, 0)))

    qi = post[:, C_QI * LANE:(C_QI + IDX_HEADS) * LANE].reshape(batch, t_new, IDX_HEADS, LANE)
    qi = qi.transpose(0, 2, 1, 3).reshape(batch, IDX_HEADS * t_new, LANE)
    wi = pf32[:, F_WIDX * LANE:F_WIDX * LANE + IDX_HEADS].reshape(batch, t_new, IDX_HEADS)
    wi = jnp.broadcast_to(wi.transpose(0, 2, 1).reshape(batch, IDX_HEADS * t_new, 1), (batch, IDX_HEADS * t_new, LANE))
    kidx_new = new_page(pf32[:, F_KIDX * LANE:(F_KIDX + 1) * LANE], 1).reshape(batch, page, LANE)

    phys = lambda b, p, pt: pt[b, jnp.minimum(p, n_pages - 1)]
    per_b = lambda b, p, pt: (b, 0, 0)
    per_b4 = lambda b, p, pt: (b, 0, 0, 0)
    sel = pl.pallas_call(
        functools.partial(_sample_index_kernel, n_pages=n_pages, page=page, t_new=t_new, topk=topk),
        grid_spec=pltpu.PrefetchScalarGridSpec(
            num_scalar_prefetch=1, grid=(batch, n_pages + 1),
            in_specs=[
                pl.BlockSpec((1, 1, page, LANE), lambda b, p, pt: (layer, phys(b, p, pt), 0, 0)),
                pl.BlockSpec((1, page, LANE), per_b),
                pl.BlockSpec((1, IDX_HEADS * t_new, LANE), per_b),
                pl.BlockSpec((1, IDX_HEADS * t_new, LANE), per_b),
            ],
            out_specs=pl.BlockSpec((1, t_new, width), per_b),
            scratch_shapes=[pltpu.VMEM((t_new, width), I32)]),
        out_shape=jax.ShapeDtypeStruct((batch, t_new, width), F32),
        compiler_params=_cparams("parallel", "arbitrary"),
        name="sample_index",
    )(page_table, cache_ki, kidx_new, qi, wi)

    def group_rows(x, heads, g):
        x = x.reshape(batch, t_new, KVH, g, LANE).transpose(0, 2, 3, 1, 4).reshape(batch, KVH, g * t_new, LANE)
        return jnp.pad(x, ((0, 0), (0, 0), (0, 32 - g * t_new), (0, 0))).reshape(batch, KVH * 32, LANE)

    qa = group_rows(post[:, C_QA * LANE:(C_QA + HEADS_A) * LANE], HEADS_A, GA)
    qb = group_rows(post[:, C_QB * LANE:(C_QB + HEADS_B) * LANE], HEADS_B, GB)
    qc = post[:, C_QC * LANE:(C_QC + HEADS_C) * LANE].reshape(batch, t_new, KVH, GC, LANE)
    qc = qc.transpose(0, 2, 3, 1, 4).reshape(batch, KVH, 1, GC * t_new, LANE)
    lo = (jnp.arange(LANE) < HALF_DIM)
    qc = jnp.concatenate([jnp.where(lo, qc, 0), jnp.where(lo, 0, qc)], axis=2).reshape(batch, KVH * 32, LANE)
    k_new = new_page(pf32[:, F_K * LANE:(F_K + n_kv) * LANE], n_kv)
    v_new = new_page(pf32[:, F_V * LANE:(F_V + n_kv) * LANE], n_kv)
    kv_page = pl.BlockSpec((1, 1, n_kv, page, LANE), lambda b, p, pt: (layer, phys(b, p, pt), 0, 0, 0))
    kv_new = pl.BlockSpec((1, n_kv, page, LANE), per_b4)

    rows = KVH * 32
    out_rows = 2 * rows + KVH * GC * t_new
    vm = lambda *s: pltpu.VMEM(s, F32)
    o = pl.pallas_call(
        functools.partial(_sample_attn_kernel, n_pages=n_pages, page=page, t_new=t_new, ksel=ksel, lam_init=lam_init),
        grid_spec=pltpu.PrefetchScalarGridSpec(
            num_scalar_prefetch=1, grid=(batch, n_pages + 1),
            in_specs=[
                kv_page, kv_page, kv_new, kv_new,
                pl.BlockSpec((1, 1, t_new, page), lambda b, p, pt: (b, p, 0, 0)),
                pl.BlockSpec((1, rows, LANE), per_b),
                pl.BlockSpec((1, rows, LANE), per_b),
                pl.BlockSpec((1, rows, LANE), per_b),
                pl.BlockSpec(lam.shape, lambda b, p, pt: (0, 0)),
                pl.BlockSpec((1, LANE), lambda b, p, pt: (0, 0)),
            ],
            out_specs=pl.BlockSpec((1, out_rows, LANE), per_b),
            scratch_shapes=[
                vm(rows, 1), vm(rows, 1), vm(rows, LANE),
                vm(rows, 1), vm(rows, 1), vm(rows, LANE),
                vm(n_blk, rows, 1), vm(n_blk, rows, 1), vm(n_blk, rows, LANE),
                vm(1, KVH * LANE), vm(LANE, KVH * LANE),
                vm(rows, 1), vm(rows, 1), vm(rows, LANE),
            ]),
        out_shape=jax.ShapeDtypeStruct((batch, out_rows, LANE), F32),
        compiler_params=_cparams("parallel", "arbitrary"),
        name="sample_attn",
    )(page_table, cache_k.transpose(0, 1, 3, 2, 4), cache_v.transpose(0, 1, 3, 2, 4), k_new, v_new,
      sel.reshape(batch, t_new, n_pages + 1, page).transpose(0, 2, 1, 3), qa, qb, qc, lam, g_sub.reshape(1, LANE))

    def ungroup(x, g):
        x = x.reshape(batch, KVH, 32, LANE)[:, :, :g * t_new].reshape(batch, KVH, g, t_new, LANE)
        return x.transpose(0, 3, 1, 2, 4).reshape(batch * t_new, KVH * g * LANE)

    o_a = ungroup(o[:, 0:rows], GA)
    o_b = ungroup(o[:, rows:2 * rows], GB)
    o_c = o[:, 2 * rows:].reshape(batch, KVH, GC, t_new, LANE).transpose(0, 3, 1, 2, 4).reshape(batch * t_new, HEADS_C * LANE)
    return o_a.astype(BF16), o_b.astype(BF16), o_c.astype(BF16)


def _rope_tables(pos):
    t = pos.shape[0]
    posf = pos.astype(F32)[:, None]
    a64 = posf * (ROPE_THETA ** (-jnp.arange(64, dtype=F32) / 64))[None, :]
    a32 = posf * (ROPE_THETA ** (-jnp.arange(32, dtype=F32) / 32))[None, :]
    c64, s64, c32, s32 = jnp.cos(a64), jnp.sin(a64), jnp.cos(a32), jnp.sin(a32)
    one, zero = jnp.ones((t, LANE), F32), jnp.zeros((t, LANE), F32)
    z32, z64, o64 = jnp.zeros((t, 32), F32), jnp.zeros((t, 64), F32), jnp.ones((t, 64), F32)
    cat = lambda *xs: jnp.concatenate(xs, axis=1)
    kinds = (
        (one, zero, zero, zero),
        (cat(c64, c64), cat(-s64, s64), zero, zero),
        (cat(c32, c32, c32, c32), zero, cat(-s32, z32, -s32, z32), cat(z32, s32, z32, s32)),
        (cat(c32, c32, o64), zero, cat(-s32, z32, z64), cat(z32, s32, z64)),
    )
    return jnp.stack([jnp.stack(k) for k in kinds])


def _in_meta():
    rows = []
    rows += [(1, 1, 1)] * HEADS_A
    rows += [(1, 3, 1)] * HEADS_B
    rows += [(2, 5, 2)] * HEADS_C
    rows += [(0, 0, 3)] * IDX_HEADS
    rows += [(1, 2, 1)] * KVH + [(1, 4, 1)] * KVH + [(2, 6, 2)] * KVH
    rows += [(0, 0, 0)] * (3 * KVH)
    rows += [(1, 7, 3), (0, 0, 0), (0, 0, 0), (0, 0, 0)]
    assert len(rows) == N_CHUNKS
    return jnp.asarray(rows, I32).T


def _in_weight(w_in):
    d = w_in.shape[0]
    n_q = (HEADS_A + HEADS_B + HEADS_C) * LANE
    n_kv = 2 * 3 * KVH * LANE
    n_qi = IDX_HEADS * IDX_DIM
    q, kv = w_in[:, :n_q], w_in[:, n_q:n_q + n_kv]
    qi = w_in[:, n_q + n_kv:n_q + n_kv + n_qi]
    widx = w_in[:, n_q + n_kv + n_qi:n_q + n_kv + n_qi + IDX_HEADS]
    kidx = w_in[:, n_q + n_kv + n_qi + IDX_HEADS:]
    pad = jnp.zeros((d, N_CHUNKS * LANE - w_in.shape[1]), w_in.dtype)
    return jnp.concatenate([q, qi, kv, kidx, widx, pad], axis=1).astype(BF16)


def kernel(x_prompt, x_sample, mem_prompt, cache_k, cache_v, cache_k_idx, cache_mem_k, cache_mem_v, page_table,
           w_in, g_qk_a, g_qk_b, g_qk_c, g_idx_k, lam, g_sub, w_o, g_mix, g_cross, g_mem, w_cq, w_ckv, w_co,
           g_qk_x, g_mlp, w_up, w_down):
    depth = w_in.shape[0]
    bp, tp, d = x_prompt.shape
    bs, ts, _ = x_sample.shape
    mem = mem_prompt.shape[1]
    n_pages = page_table.shape[1]
    page = cache_k.shape[2]
    past = n_pages * page
    mp, ms = bp * tp, bs * ts
    mem_w = MEM_HEADS * HEAD_DIM
    kv_w = 3 * KVH * LANE
    tm_p = 512

    xp = x_prompt.reshape(mp, d)
    xs = x_sample.reshape(ms, d)
    memx = mem_prompt.reshape(bp * mem, d)
    tab_p = _rope_tables(jnp.arange(tp, dtype=I32))
    tab_s = _rope_tables(jnp.tile(past + jnp.arange(ts, dtype=I32), bs))
    tab_id = _rope_tables(jnp.zeros((256,), I32))[:1]
    in_meta = _in_meta()
    cq_meta = jnp.asarray([(1, 0, 0)] * 4, I32).T
    ckv_meta = jnp.asarray([(1, 1, 0)] * 4 + [(0, 0, 0)] * 4, I32).T
    ones_row = jnp.ones((LANE,), F32)

    outs = {k: [] for k in ("kp", "vp", "ip", "mkp", "mvp", "ks", "vs", "is")}
    for l in range(depth):
        lam_init = 0.8 - 0.6 * math.exp(-0.3 * l)
        w_in_bf = _in_weight(w_in[l])
        wo_bf = w_o[l].astype(BF16)
        wo_parts = [wo_bf[:HEADS_A * LANE], wo_bf[HEADS_A * LANE:(HEADS_A + HEADS_B) * LANE],
                    wo_bf[(HEADS_A + HEADS_B) * LANE:]]
        wcq_bf, wckv_bf, wco_bf = w_cq[l].astype(BF16), w_ckv[l].astype(BF16), w_co[l].astype(BF16)
        wup_bf, wdn_bf = w_up[l].astype(BF16), w_down[l].astype(BF16)
        gains = jnp.stack([jnp.ones((LANE,), F32), g_qk_a[l, 0], g_qk_a[l, 1], g_qk_b[l, 0], g_qk_b[l, 1],
                           jnp.tile(g_qk_c[l, 0], 2), jnp.tile(g_qk_c[l, 1], 2), g_idx_k[l]])

        mkv_bf, mkv_f32 = _proj(memx, g_mem[l], wckv_bf, ckv_meta, jnp.stack([ones_row, g_qk_x[l, 1]]), tab_id,
                                tm=256, n_nonf32=0, n_pos_tiles=1, name="mem_kv")
        post, pf32 = _proj(xp, g_mix[l], w_in_bf, in_meta, gains, tab_p,
                           tm=tm_p, n_nonf32=N_NONF32_TILES, n_pos_tiles=tp // tm_p, name="in_proj_p")
        o_parts = _prompt_mixers(post, pf32, lam[l], g_sub[l], lam_init, batch=bp, t_len=tp)
        xp = _resmm(xp, list(o_parts), wo_parts, tm=512, tn=512, name="wo_p")
        qx, _ = _proj(xp, g_cross[l], wcq_bf, cq_meta, g_qk_x[l, 0].reshape(1, LANE), tab_id,
                      tm=256, n_nonf32=1, n_pos_tiles=1, name="cq_p")
        oc = _cross(qx, mkv_bf[:, :mem_w], mkv_bf[:, mem_w:], batch=bp, t_len=tp, tq=TQ, out_dtype=BF16, name="cross_p")
        xp = _resmm(xp, [oc], [wco_bf], tm=512, tn=512, name="wco_p")
        xp = _mlp(xp, g_mlp[l], wup_bf, wdn_bf, tm=256, tf=512, name="mlp_p")
        outs["kp"].append(pf32[:, F_K * LANE:F_K * LANE + kv_w].reshape(bp, tp, 3 * KVH, HEAD_DIM))
        outs["vp"].append(pf32[:, F_V * LANE:F_V * LANE + kv_w].reshape(bp, tp, 3 * KVH, HEAD_DIM))
        outs["ip"].append(pf32[:, F_KIDX * LANE:(F_KIDX + 1) * LANE].reshape(bp, tp, IDX_DIM))
        outs["mkp"].append(mkv_f32[:, :mem_w].reshape(bp, mem, MEM_HEADS, HEAD_DIM))
        outs["mvp"].append(mkv_f32[:, mem_w:].reshape(bp, mem, MEM_HEADS, HEAD_DIM))

        post_s, pf32_s = _proj(xs, g_mix[l], w_in_bf, in_meta, gains, tab_s,
                               tm=ms, n_nonf32=N_NONF32_TILES, n_pos_tiles=1, name="in_proj_s")
        o_parts = _sample_mixers(post_s, pf32_s, cache_k, cache_v, cache_k_idx, l, page_table,
                                 lam[l], g_sub[l], lam_init, batch=bs, t_new=ts)
        xs = _resmm(xs, list(o_parts), wo_parts, tm=ms, tn=512, name="wo_s")
        _, qx_s = _proj(xs, g_cross[l], wcq_bf, cq_meta, g_qk_x[l, 0].reshape(1, LANE), tab_id[:, :, :ms],
                        tm=ms, n_nonf32=0, n_pos_tiles=1, name="cq_s")
        oc_s = _cross(qx_s, cache_mem_k[l].reshape(bs * mem, mem_w).astype(BF16),
                      cache_mem_v[l].reshape(bs * mem, mem_w).astype(BF16),
                      batch=bs, t_len=ts, tq=ts, out_dtype=F32, name="cross_s")
        xs = _resmm(xs, [oc_s.astype(BF16)], [wco_bf], tm=ms, tn=512, name="wco_s")
        xs = _mlp(xs, g_mlp[l], wup_bf, wdn_bf, tm=ms, tf=512, name="mlp_s")
        outs["ks"].append(pf32_s[:, F_K * LANE:F_K * LANE + kv_w].reshape(bs, ts, 3 * KVH, HEAD_DIM))
        outs["vs"].append(pf32_s[:, F_V * LANE:F_V * LANE + kv_w].reshape(bs, ts, 3 * KVH, HEAD_DIM))
        outs["is"].append(pf32_s[:, F_KIDX * LANE:(F_KIDX + 1) * LANE].reshape(bs, ts, IDX_DIM))

    st = lambda k: jnp.stack(outs[k])
    return (xp.reshape(bp, tp, d), xs.reshape(bs, ts, d), st("kp"), st("vp"), st("ip"), st("mkp"), st("mvp"),
            st("ks"), st("vs"), st("is"))
```
